```python
import jax, jax.numpy as jnp
from jax import lax
import numpy as np

D_MODEL = 1024
BATCH = 8
SEQ = 2048
DEPTH = 1
DEC_BATCH = 128
DEC_SEQ = 1
PAST_LEN = 16384
PAGE_SIZE = 128

D_CONV = D_MODEL
CONV_K = 31
GLA_HEADS = 4
GLA_DK = D_MODEL // 2
GLA_DV = D_MODEL
GLA_HK = GLA_DK // GLA_HEADS
GLA_HV = GLA_DV // GLA_HEADS
GLA_RANK = 16
GLA_TAU = 16.0
GLA_CHUNK = 64
D_FF = ((8 * D_MODEL // 3 + 255) // 256) * 256
N_ADA = 9
EPS = 1e-6
IN_SPLITS = (D_CONV, D_CONV, GLA_DK, GLA_DK, GLA_DV, GLA_DV, GLA_RANK, D_MODEL, D_MODEL)
N_IN = sum(IN_SPLITS)

kernel_name = 'hybrid_conformer_gla_decoder_step'


def rms_norm(x, g):
    xf = x.astype(jnp.float32)
    y = xf * lax.rsqrt(jnp.mean(xf * xf, axis=-1, keepdims=True) + EPS)
    return (y * g.astype(jnp.float32)).astype(x.dtype)


def layer_norm(x, g, b):
    xf = x.astype(jnp.float32)
    mu = jnp.mean(xf, axis=-1, keepdims=True)
    var = jnp.mean(jnp.square(xf - mu), axis=-1, keepdims=True)
    y = (xf - mu) * lax.rsqrt(var + EPS) * g.astype(jnp.float32) + b.astype(jnp.float32)
    return y.astype(x.dtype)


def modulate(h, shift, scale):
    return h * (1 + scale[:, None, :]) + shift[:, None, :]


def swiglu(h, w_gate, w_up, w_down):
    return (jax.nn.silu(h @ w_gate) * (h @ w_up)) @ w_down


def causal_depthwise_conv(u, buf, w, b):
    upad = jnp.concatenate([buf.astype(u.dtype), u], axis=1)
    y = lax.conv_general_dilated(upad, w[:, None, :].astype(u.dtype), window_strides=(1,), padding='VALID',
                                 dimension_numbers=('NWC', 'WIO', 'NWC'), feature_group_count=u.shape[-1])
    return y + b, upad[:, -(CONV_K - 1):]


def gla_chunked(q, k, v, log_a, s0):
    bsz, t_len, n_h, _ = q.shape
    blk = min(GLA_CHUNK, t_len)
    n_c = -(-t_len // blk)
    pad = n_c * blk - t_len

    def prep(a):
        a = jnp.pad(a.astype(jnp.float32), ((0, 0), (0, pad), (0, 0), (0, 0)))
        return a.reshape(bsz, n_c, blk, n_h, a.shape[-1]).transpose(1, 0, 3, 2, 4)

    mask = jnp.tril(jnp.ones((blk, blk), dtype=bool))

    def step(s, inp):
        qc, kc, vc, lac = inp
        bc = jnp.cumsum(lac, axis=2)
        o_inter = jnp.einsum('bhtk,bhkv->bhtv', qc * jnp.exp(bc), s)
        diff = bc[:, :, :, None, :] - bc[:, :, None, :, :]
        decay = jnp.exp(jnp.where(mask[:, :, None], diff, -jnp.inf))
        scores = jnp.einsum('bhtk,bhsk,bhtsk->bhts', qc, kc, decay)
        o_intra = jnp.einsum('bhts,bhsv->bhtv', scores, vc)
        b_last = bc[:, :, -1:, :]
        s_new = jnp.exp(b_last[:, :, 0, :, None]) * s + jnp.einsum('bhsk,bhsv->bhkv', kc * jnp.exp(b_last - bc), vc)
        return s_new, o_inter + o_intra

    s_fin, o = lax.scan(step, s0.astype(jnp.float32), (prep(q), prep(k), prep(v), prep(log_a)))
    o = o.transpose(1, 0, 3, 2, 4).reshape(bsz, n_c * blk, n_h, v.shape[-1])[:, :t_len]
    return o, s_fin


def decoder(x, c, conv_state, gla_state, w_ada, b_ada, norm_ffn1, ffn1_gate, ffn1_up, ffn1_down,
            norm_mix, w_in, conv_w, conv_b, conv_ln_g, conv_ln_b, w_conv_out, w_alpha, b_alpha,
            gla_norm, w_gla_out, w_out, norm_ffn2, ffn2_gate, ffn2_up, ffn2_down,
            norm_final, w_ada_final, b_ada_final):
    bsz, t_len, _ = x.shape
    c_act = jax.nn.silu(c)
    split_idx = np.cumsum(IN_SPLITS)[:-1]
    new_conv, new_gla = [], []
    for i in range(DEPTH):
        mod = c_act @ w_ada[i] + b_ada[i]
        sh1, sc1, gt1, sh2, sc2, gt2, sh3, sc3, gt3 = jnp.split(mod, N_ADA, axis=-1)
        h = modulate(rms_norm(x, norm_ffn1[i]), sh1, sc1)
        x = x + 0.5 * gt1[:, None, :] * swiglu(h, ffn1_gate[i], ffn1_up[i], ffn1_down[i])
        h = modulate(rms_norm(x, norm_mix[i]), sh2, sc2)
        z = h @ w_in[i]
        ca, cb, q, k, v, r, a_lr, g_conv, g_gla = jnp.split(z, split_idx, axis=-1)
        u = ca * jax.nn.sigmoid(cb)
        d, conv_new = causal_depthwise_conv(u, conv_state[i], conv_w[i], conv_b[i])
        y_conv = jax.nn.silu(layer_norm(d, conv_ln_g[i], conv_ln_b[i])) @ w_conv_out[i]
        log_a = jax.nn.log_sigmoid((a_lr @ w_alpha[i] + b_alpha[i]).astype(jnp.float32)) / GLA_TAU
        qh = q.reshape(bsz, t_len, GLA_HEADS, GLA_HK) * (GLA_HK ** -0.5)
        kh = k.reshape(bsz, t_len, GLA_HEADS, GLA_HK)
        vh = v.reshape(bsz, t_len, GLA_HEADS, GLA_HV)
        lah = log_a.reshape(bsz, t_len, GLA_HEADS, GLA_HK)
        o, s_new = gla_chunked(qh, kh, vh, lah, gla_state[i])
        o = rms_norm(o, gla_norm[i].reshape(GLA_HEADS, GLA_HV)).reshape(bsz, t_len, GLA_DV).astype(x.dtype)
        y_gla = (o * jax.nn.silu(r)) @ w_gla_out[i]
        merged = jax.nn.sigmoid(g_conv) * y_conv + jax.nn.sigmoid(g_gla) * y_gla
        x = x + gt2[:, None, :] * (merged @ w_out[i])
        h = modulate(rms_norm(x, norm_ffn2[i]), sh3, sc3)
        x = x + 0.5 * gt3[:, None, :] * swiglu(h, ffn2_gate[i], ffn2_up[i], ffn2_down[i])
        new_conv.append(conv_new.astype(conv_state.dtype))
        new_gla.append(s_new.astype(gla_state.dtype))
    shf, scf = jnp.split(c_act @ w_ada_final + b_ada_final, 2, axis=-1)
    y = modulate(rms_norm(x, norm_final), shf, scf)
    return y, jnp.stack(new_conv), jnp.stack(new_gla)


def setup_inputs(seed: int = 0) -> dict:
    key = jax.random.key(seed)
    ks = jax.random.split(key, 40)

    def nrm(k, shape, scale):
        return jax.random.normal(k, shape, dtype=jnp.float32) * scale

    def gain(k, shape):
        return 1.0 + nrm(k, shape, 0.02)

    L = DEPTH
    return {
        'x_prompt': nrm(ks[0], (BATCH, SEQ, D_MODEL), 1.0),
        'x_sample': nrm(ks[1], (DEC_BATCH, DEC_SEQ, D_MODEL), 1.0),
        'c_prompt': nrm(ks[2], (BATCH, D_MODEL), 1.0),
        'c_sample': nrm(ks[3], (DEC_BATCH, D_MODEL), 1.0),
        'state_conv': nrm(ks[4], (L, DEC_BATCH, CONV_K - 1, D_CONV), 0.5),
        'state_gla': nrm(ks[5], (L, DEC_BATCH, GLA_HEADS, GLA_HK, GLA_HV), 0.5),
        'w_ada': nrm(ks[6], (L, D_MODEL, N_ADA * D_MODEL), 0.5 * D_MODEL ** -0.5),
        'b_ada': nrm(ks[7], (L, N_ADA * D_MODEL), 0.01),
        'norm_ffn1': gain(ks[8], (L, D_MODEL)),
        'ffn1_gate': nrm(ks[9], (L, D_MODEL, D_FF), D_MODEL ** -0.5),
        'ffn1_up': nrm(ks[10], (L, D_MODEL, D_FF), D_MODEL ** -0.5),
        'ffn1_down': nrm(ks[11], (L, D_FF, D_MODEL), D_FF ** -0.5),
        'norm_mix': gain(ks[12], (L, D_MODEL)),
        'w_in': nrm(ks[13], (L, D_MODEL, N_IN), D_MODEL ** -0.5),
        'conv_w': nrm(ks[14], (L, CONV_K, D_CONV), CONV_K ** -0.5),
        'conv_b': nrm(ks[15], (L, D_CONV), 0.02),
        'conv_ln_g': gain(ks[16], (L, D_CONV)),
        'conv_ln_b': nrm(ks[17], (L, D_CONV), 0.02),
        'w_conv_out': nrm(ks[18], (L, D_CONV, D_MODEL), D_CONV ** -0.5),
        'w_alpha': nrm(ks[19], (L, GLA_RANK, GLA_DK), GLA_RANK ** -0.5),
        'b_alpha': nrm(ks[20], (L, GLA_DK), 0.1),
        'gla_norm': gain(ks[21], (L, GLA_DV)),
        'w_gla_out': nrm(ks[22], (L, GLA_DV, D_MODEL), GLA_DV ** -0.5),
        'w_out': nrm(ks[23], (L, D_MODEL, D_MODEL), D_MODEL ** -0.5),
        'norm_ffn2': gain(ks[24], (L, D_MODEL)),
        'ffn2_gate': nrm(ks[25], (L, D_MODEL, D_FF), D_MODEL ** -0.5),
        'ffn2_up': nrm(ks[26], (L, D_MODEL, D_FF), D_MODEL ** -0.5),
        'ffn2_down': nrm(ks[27], (L, D_FF, D_MODEL), D_FF ** -0.5),
        'norm_final': gain(ks[28], (D_MODEL,)),
        'w_ada_final': nrm(ks[29], (D_MODEL, 2 * D_MODEL), 0.5 * D_MODEL ** -0.5),
        'b_ada_final': nrm(ks[30], (2 * D_MODEL,), 0.01),
    }


def reference(x_prompt, x_sample, c_prompt, c_sample, state_conv, state_gla,
              w_ada, b_ada, norm_ffn1, ffn1_gate, ffn1_up, ffn1_down,
              norm_mix, w_in, conv_w, conv_b, conv_ln_g, conv_ln_b, w_conv_out, w_alpha, b_alpha,
              gla_norm, w_gla_out, w_out, norm_ffn2, ffn2_gate, ffn2_up, ffn2_down,
              norm_final, w_ada_final, b_ada_final):
    weights = (w_ada, b_ada, norm_ffn1, ffn1_gate, ffn1_up, ffn1_down,
               norm_mix, w_in, conv_w, conv_b, conv_ln_g, conv_ln_b, w_conv_out, w_alpha, b_alpha,
               gla_norm, w_gla_out, w_out, norm_ffn2, ffn2_gate, ffn2_up, ffn2_down,
               norm_final, w_ada_final, b_ada_final)
    bp = x_prompt.shape[0]
    zero_conv = jnp.zeros((DEPTH, bp, CONV_K - 1, D_CONV), dtype=state_conv.dtype)
    zero_gla = jnp.zeros((DEPTH, bp, GLA_HEADS, GLA_HK, GLA_HV), dtype=state_gla.dtype)
    y_prompt, conv_p, gla_p = decoder(x_prompt, c_prompt, zero_conv, zero_gla, *weights)
    y_sample, conv_s, gla_s = decoder(x_sample, c_sample, state_conv, state_gla, *weights)
    return (y_prompt, y_sample, conv_p, gla_p, conv_s, gla_s)
```

```python
import functools

import jax
import jax.numpy as jnp
from jax import lax
from jax.experimental import pallas as pl
from jax.experimental.pallas import tpu as pltpu

D_MODEL = 1024
CONV_K = 31
GLA_HEADS = 4
GLA_HK = 128
GLA_HV = 256
GLA_DK = GLA_HEADS * GLA_HK
GLA_DV = GLA_HEADS * GLA_HV
GLA_RANK = 16
GLA_TAU = 16.0
D_FF = 2816
N_ADA = 9
EPS = 1e-6
IN_SPLITS = (D_MODEL, D_MODEL, GLA_DK, GLA_DK, GLA_DV, GLA_DV, GLA_RANK, D_MODEL, D_MODEL)

LANES = 128
SUBLANES = 8
VMEM_LIMIT_BYTES = 56 * 2**20
FFN_ROWS = 512
FFN_COLS = 256
MIX_ROWS = 256
GLA_CHUNK = 128
CONV_HALO = 32
ADA_COLS = 1024
CONV_S_ROWS = 16
GLA_S_ROWS = 8

f32 = jnp.float32
bf16 = jnp.bfloat16


def _dot(a, b):
    return jnp.dot(a, b, preferred_element_type=f32)


def _sigmoid(x):
    return 1.0 / (1.0 + jnp.exp(-x))


def _silu(x):
    return x * _sigmoid(x)


def _rms_norm(x, g):
    return x * lax.rsqrt(jnp.mean(x * x, axis=-1, keepdims=True) + EPS) * g


def _layer_norm(x, g, b):
    mu = jnp.mean(x, axis=-1, keepdims=True)
    xc = x - mu
    return xc * lax.rsqrt(jnp.mean(xc * xc, axis=-1, keepdims=True) + EPS) * g + b


def _const_spec(shape):
    return pl.BlockSpec(shape, lambda *_: (0,) * len(shape), pipeline_mode=pl.Buffered(1))


def _params(n_grid_axes):
    return pltpu.CompilerParams(dimension_semantics=("arbitrary",) * n_grid_axes, vmem_limit_bytes=VMEM_LIMIT_BYTES)


def _ada_kernel(c_ref, w_ref, b_ref, o_ref):
    c_act = _silu(c_ref[...]).astype(bf16)
    o_ref[0] = _dot(c_act, w_ref[...].astype(bf16)) + b_ref[0]


def _ada(c, w, b):
    rows = c.shape[0]
    n = w.shape[1] // ADA_COLS
    return pl.pallas_call(
        _ada_kernel,
        grid=(n,),
        in_specs=[_const_spec((rows, D_MODEL)),
                  pl.BlockSpec((D_MODEL, ADA_COLS), lambda j: (0, j)),
                  pl.BlockSpec((1, 1, ADA_COLS), lambda j: (j, 0, 0))],
        out_specs=pl.BlockSpec((1, rows, ADA_COLS), lambda j: (j, 0, 0)),
        out_shape=jax.ShapeDtypeStruct((n, rows, ADA_COLS), f32),
        compiler_params=_params(1),
        name="ada",
    )(c, w, b.reshape(n, 1, ADA_COLS))


def _ffn_kernel(x_ref, sh_ref, sc_ref, gt_ref, g_ref, wg_ref, wu_ref, wd_ref, *rest, final):
    if final:
        gf_ref, shf_ref, scf_ref, o_ref, a_scr = rest
    else:
        o_ref, a_scr = rest
    x = x_ref[0]
    h = (_rms_norm(x, g_ref[...]) * (1.0 + sc_ref[0]) + sh_ref[0]).astype(bf16)
    for c in range(D_FF // FFN_COLS):
        sl = slice(c * FFN_COLS, (c + 1) * FFN_COLS)
        gate = _dot(h, wg_ref[:, sl])
        up = _dot(h, wu_ref[:, sl])
        a_scr[:, sl] = (_silu(gate) * up).astype(bf16)
    y = x + 0.5 * gt_ref[0] * _dot(a_scr[...], wd_ref[...])
    if final:
        y = _rms_norm(y, gf_ref[...]) * (1.0 + scf_ref[0]) + shf_ref[0]
    o_ref[0] = y


def _ffn(x, mods, norm_g, wg, wu, wd, rows, final_args=None):
    nb, t, _ = x.shape
    tm_mod = mods[0].shape[1]
    mod_rows = 1 if tm_mod == 1 else rows
    mod_idx = (lambda b, i: (b, 0, 0)) if tm_mod == 1 else (lambda b, i: (b, i, 0))
    x_spec = pl.BlockSpec((1, rows, D_MODEL), lambda b, i: (b, i, 0))
    mod_spec = pl.BlockSpec((1, mod_rows, D_MODEL), mod_idx)
    in_specs = [x_spec, mod_spec, mod_spec, mod_spec, _const_spec((1, D_MODEL)),
                _const_spec((D_MODEL, D_FF)), _const_spec((D_MODEL, D_FF)), _const_spec((D_FF, D_MODEL))]
    args = [x, *mods, norm_g, wg, wu, wd]
    if final_args is not None:
        in_specs += [_const_spec((1, D_MODEL)), mod_spec, mod_spec]
        args += list(final_args)
    return pl.pallas_call(
        functools.partial(_ffn_kernel, final=final_args is not None),
        grid=(nb, t // rows),
        in_specs=in_specs,
        out_specs=x_spec,
        out_shape=jax.ShapeDtypeStruct(x.shape, f32),
        scratch_shapes=[pltpu.VMEM((rows, D_FF), bf16)],
        compiler_params=_params(2),
        name="ffn_final" if final_args is not None else "ffn",
    )(*args)


def _log_decay(alr, w_alpha, b_alpha):
    x = _dot(alr.astype(bf16), w_alpha) + b_alpha
    return (jnp.minimum(x, 0.0) - jnp.log1p(jnp.exp(-jnp.abs(x)))) * (1.0 / GLA_TAU)


def _in_proj(h, w):
    u = _dot(h, w["ca"][...]) * _sigmoid(_dot(h, w["cb"][...]))
    q = _dot(h, w["q"][...]) * (GLA_HK ** -0.5)
    k = _dot(h, w["k"][...])
    v = _dot(h, w["v"][...])
    rs = _silu(_dot(h, w["r"][...]))
    la = _log_decay(_dot(h, w["alr"][...]), w["alpha"][...], w["b_alpha"][...])
    gc = _sigmoid(_dot(h, w["gc"][...]))
    gg = _sigmoid(_dot(h, w["gg"][...]))
    return u, q, k, v, la, rs, gc, gg


def _out_proj(x, d, o, rs, gc, gg, gt, w):
    y_conv = _dot(_silu(_layer_norm(d, w["ln_g"][...], w["ln_b"][...])).astype(bf16), w["conv_out"][...])
    gn = w["gla_norm"][...]
    o = jnp.concatenate(
        [_rms_norm(o[:, h * GLA_HV:(h + 1) * GLA_HV], gn[:, h * GLA_HV:(h + 1) * GLA_HV]) for h in range(GLA_HEADS)],
        axis=1)
    y_gla = _dot((o * rs).astype(bf16), w["gla_out"][...])
    merged = gc * y_conv + gg * y_gla
    return x + gt * _dot(merged.astype(bf16), w["out"][...])


_IN_W = ("ca", "cb", "q", "k", "v", "r", "alr", "gc", "gg", "alpha", "b_alpha")
_OUT_W = ("ln_g", "ln_b", "conv_out", "gla_norm", "gla_out", "out")


def _w_specs(w, names):
    return [_const_spec(w[n].shape) for n in names]


def _pair_levels(c):
    t = lax.broadcasted_iota(jnp.int32, (c, c), 0)
    s = lax.broadcasted_iota(jnp.int32, (c, c), 1)
    x = t ^ s
    lev = jnp.zeros((c, c), jnp.int32)
    b = 2
    while b < c:
        lev = lev + (x >= b).astype(jnp.int32)
        b *= 2
    return jnp.where(t > s, lev, jnp.where(t == s, -1, -2))


def _first_half_total(p, b):
    c, n = p.shape
    if b >= SUBLANES:
        p3 = p.reshape(c // (2 * b), 2 * b, n)
        return jnp.broadcast_to(p3[:, b - 1:b, :], p3.shape).reshape(c, n)
    p8 = p.reshape(c // SUBLANES, SUBLANES, n)
    sub = lax.broadcasted_iota(jnp.int32, p8.shape, 1)
    if b == 4:
        r = jnp.broadcast_to(p8[:, 3:4, :], p8.shape)
    elif b == 2:
        r = jnp.where(sub < 4, jnp.broadcast_to(p8[:, 1:2, :], p8.shape), jnp.broadcast_to(p8[:, 5:6, :], p8.shape))
    else:
        r = jnp.where((sub & 1) == 0, p8, pltpu.roll(p8, 1, 1))
    return r.reshape(c, n)


def _gla_block(q, k, v, la, s_ref, lev):
    c = q.shape[0]
    heads = [(slice(h * GLA_HK, (h + 1) * GLA_HK), slice(h * GLA_HV, (h + 1) * GLA_HV)) for h in range(GLA_HEADS)]
    nt = (((1,), (1,)), ((), ()))
    row = lax.broadcasted_iota(jnp.int32, la.shape, 0)
    qb, kb, vb = q.astype(bf16), k.astype(bf16), v.astype(bf16)
    scores = [jnp.where(lev == -1, lax.dot_general(qb[:, sk], kb[:, sk], nt, preferred_element_type=f32), 0.0)
              for sk, _ in heads]
    p = la
    b, level = 1, 0
    while b < c:
        tot = _first_half_total(p, b)
        qt = (q * jnp.exp(p)).astype(bf16)
        kt = (k * jnp.exp(jnp.minimum(tot - p, 0.0))).astype(bf16)
        for h, (sk, _) in enumerate(heads):
            s = lax.dot_general(qt[:, sk], kt[:, sk], nt, preferred_element_type=f32)
            scores[h] = jnp.where(lev == level, s, scores[h])
        p = p + jnp.where((row & b) != 0, tot, 0.0)
        b, level = 2 * b, level + 1
    bc = p
    last = bc[c - 1:c, :]
    qi = (q * jnp.exp(bc)).astype(bf16)
    kd = (k * jnp.exp(jnp.broadcast_to(last, bc.shape) - bc)).astype(bf16)
    e_last = jnp.exp(last)
    eye = lax.broadcasted_iota(jnp.int32, (GLA_HK, GLA_HK), 0) == lax.broadcasted_iota(jnp.int32, (GLA_HK, GLA_HK), 1)
    outs = []
    for h, (sk, sv) in enumerate(heads):
        s_old = s_ref[h]
        o = _dot(scores[h].astype(bf16), vb[:, sv]) + _dot(qi[:, sk], s_old.astype(bf16))
        e_col = jnp.sum(jnp.where(eye, jnp.broadcast_to(e_last[:, sk], (GLA_HK, GLA_HK)), 0.0), axis=1, keepdims=True)
        s_ref[h] = s_old * e_col + lax.dot_general(kd[:, sk], vb[:, sv], (((0,), (0,)), ((), ())),
                                                   preferred_element_type=f32)
        outs.append(o)
    return jnp.concatenate(outs, axis=1)


def _conv_tile(ubuf, w_ref, b_ref, rows):
    acc = None
    for c in range(SUBLANES):
        z = None
        for a in range(-(-CONV_K // SUBLANES)):
            lag = SUBLANES * a + c
            if lag >= CONV_K:
                continue
            start = CONV_HALO - SUBLANES - SUBLANES * a
            term = ubuf[start:start + rows + SUBLANES, :] * w_ref[CONV_K - 1 - lag:CONV_K - lag, :]
            z = term if z is None else z + term
        zs = z[SUBLANES - c:SUBLANES - c + rows]
        acc = zs if acc is None else acc + zs
    return acc + b_ref[...]


def _mix_kernel(*refs):
    x_ref, sh_ref, sc_ref, gt_ref, g_ref, cw_ref, cb_ref = refs[:7]
    n_in, n_out = len(_IN_W), len(_OUT_W)
    w = dict(zip(_IN_W + _OUT_W, refs[7:7 + n_in + n_out]))
    o_ref, conv_ref, gla_ref, ubuf = refs[7 + n_in + n_out:]
    i = pl.program_id(1)
    rows = x_ref.shape[1]
    s_ref = gla_ref.at[0]

    @pl.when(i == 0)
    def _():
        ubuf[0:CONV_HALO, :] = jnp.zeros((CONV_HALO, D_MODEL), f32)
        gla_ref[...] = jnp.zeros_like(gla_ref)

    x = x_ref[0]
    h = (_rms_norm(x, g_ref[...]) * (1.0 + sc_ref[0]) + sh_ref[0]).astype(bf16)
    u, q, k, v, la, rs, gc, gg = _in_proj(h, w)
    ubuf[CONV_HALO:CONV_HALO + rows, :] = u
    d = _conv_tile(ubuf, cw_ref, cb_ref, rows)
    lev = _pair_levels(GLA_CHUNK)
    o = jnp.concatenate(
        [_gla_block(q[r:r + GLA_CHUNK], k[r:r + GLA_CHUNK], v[r:r + GLA_CHUNK], la[r:r + GLA_CHUNK], s_ref, lev)
         for r in range(0, rows, GLA_CHUNK)], axis=0)
    o_ref[0] = _out_proj(x, d, o, rs, gc, gg, gt_ref[0], w)
    ubuf[0:CONV_HALO, :] = ubuf[rows:rows + CONV_HALO, :]

    @pl.when(i == pl.num_programs(1) - 1)
    def _():
        conv_ref[0] = ubuf[CONV_HALO - (CONV_K - 1):CONV_HALO, :]


def _mix_prompt(x, mods, norm_g, conv_w, conv_b, w):
    nb, t, _ = x.shape
    x_spec = pl.BlockSpec((1, MIX_ROWS, D_MODEL), lambda b, i: (b, i, 0))
    mod_spec = pl.BlockSpec((1, 1, D_MODEL), lambda b, i: (b, 0, 0))
    names = _IN_W + _OUT_W
    return pl.pallas_call(
        _mix_kernel,
        grid=(nb, t // MIX_ROWS),
        in_specs=[x_spec, mod_spec, mod_spec, mod_spec, _const_spec((1, D_MODEL)),
                  _const_spec((CONV_K, D_MODEL)), _const_spec((1, D_MODEL))] + _w_specs(w, names),
        out_specs=[x_spec,
                   pl.BlockSpec((1, CONV_K - 1, D_MODEL), lambda b, i: (b, 0, 0)),
                   pl.BlockSpec((1, GLA_HEADS, GLA_HK, GLA_HV), lambda b, i: (b, 0, 0, 0))],
        out_shape=[jax.ShapeDtypeStruct(x.shape, f32),
                   jax.ShapeDtypeStruct((nb, CONV_K - 1, D_MODEL), f32),
                   jax.ShapeDtypeStruct((nb, GLA_HEADS, GLA_HK, GLA_HV), f32)],
        scratch_shapes=[pltpu.VMEM((MIX_ROWS + CONV_HALO, D_MODEL), f32)],
        compiler_params=_params(2),
        name="mix_prompt",
    )(x, *mods, norm_g, conv_w, conv_b, *[w[n] for n in names])


def _mix_s_in_kernel(*refs):
    x_ref, sh_ref, sc_ref, g_ref = refs[:4]
    w = dict(zip(_IN_W, refs[4:4 + len(_IN_W)]))
    outs = refs[4 + len(_IN_W):]
    h = (_rms_norm(x_ref[...], g_ref[...]) * (1.0 + sc_ref[...]) + sh_ref[...]).astype(bf16)
    for o_ref, val in zip(outs, _in_proj(h, w)):
        o_ref[...] = val


def _mix_s_in(x, sh, sc, norm_g, w):
    n = x.shape[0]
    widths = (D_MODEL, GLA_DK, GLA_DK, GLA_DV, GLA_DK, GLA_DV, D_MODEL, D_MODEL)
    return pl.pallas_call(
        _mix_s_in_kernel,
        out_shape=[jax.ShapeDtypeStruct((n, wd), f32) for wd in widths],
        compiler_params=pltpu.CompilerParams(vmem_limit_bytes=VMEM_LIMIT_BYTES),
        name="mix_sample_in",
    )(x, sh, sc, norm_g, *[w[nm] for nm in _IN_W])


def _conv_s_kernel(st_ref, u_ref, w_ref, b_ref, d_ref, ns_ref):
    u = u_ref[...]
    d = b_ref[...] + u * w_ref[CONV_K - 1:CONV_K, :]
    for j in range(CONV_K - 1):
        d = d + st_ref[:, j * D_MODEL:(j + 1) * D_MODEL] * w_ref[j:j + 1, :]
    d_ref[...] = d
    ns_ref[:, 0:(CONV_K - 2) * D_MODEL] = st_ref[:, D_MODEL:(CONV_K - 1) * D_MODEL]
    ns_ref[:, (CONV_K - 2) * D_MODEL:] = u


def _conv_s(state2d, u, conv_w, conv_b):
    n, width = state2d.shape
    st_spec = pl.BlockSpec((CONV_S_ROWS, width), lambda i: (i, 0))
    row_spec = pl.BlockSpec((CONV_S_ROWS, D_MODEL), lambda i: (i, 0))
    return pl.pallas_call(
        _conv_s_kernel,
        grid=(n // CONV_S_ROWS,),
        in_specs=[st_spec, row_spec, _const_spec((CONV_K, D_MODEL)), _const_spec((1, D_MODEL))],
        out_specs=[row_spec, st_spec],
        out_shape=[jax.ShapeDtypeStruct((n, D_MODEL), f32), jax.ShapeDtypeStruct((n, width), f32)],
        compiler_params=_params(1),
        name="conv_sample",
    )(state2d, u, conv_w, conv_b)


def _gla_s_kernel(q_ref, k_ref, la_ref, v_ref, st_ref, o_ref, ns_ref):
    q, k, a, v = q_ref[...], k_ref[...], jnp.exp(la_ref[...]), v_ref[...]
    per_head = []
    for h in range(GLA_HEADS):
        sk = slice(h * GLA_HK, (h + 1) * GLA_HK)
        sv = slice(h * GLA_HV, (h + 1) * GLA_HV)
        q_t, k_t, a_t = q[:, sk].T, k[:, sk].T, a[:, sk].T
        o_rows = []
        for r in range(GLA_S_ROWS):
            s_new = st_ref[r, h] * a_t[:, r:r + 1] + k_t[:, r:r + 1] * v[r:r + 1, sv]
            ns_ref[r, h] = s_new
            o_rows.append(jnp.sum(s_new * q_t[:, r:r + 1], axis=0, keepdims=True))
        per_head.append(jnp.concatenate(o_rows, axis=0))
    o_ref[...] = jnp.concatenate(per_head, axis=1)


def _gla_s(q, k, la, v, state):
    n = q.shape[0]
    k_spec = pl.BlockSpec((GLA_S_ROWS, GLA_DK), lambda i: (i, 0))
    v_spec = pl.BlockSpec((GLA_S_ROWS, GLA_DV), lambda i: (i, 0))
    st_spec = pl.BlockSpec((GLA_S_ROWS, GLA_HEADS, GLA_HK, GLA_HV), lambda i: (i, 0, 0, 0))
    return pl.pallas_call(
        _gla_s_kernel,
        grid=(n // GLA_S_ROWS,),
        in_specs=[k_spec, k_spec, k_spec, v_spec, st_spec],
        out_specs=[v_spec, st_spec],
        out_shape=[jax.ShapeDtypeStruct((n, GLA_DV), f32), jax.ShapeDtypeStruct(state.shape, f32)],
        compiler_params=_params(1),
        name="gla_sample",
    )(q, k, la, v, state)


def _mix_s_out_kernel(*refs):
    x_ref, d_ref, o_ref_in, rs_ref, gc_ref, gg_ref, gt_ref = refs[:7]
    w = dict(zip(_OUT_W, refs[7:7 + len(_OUT_W)]))
    out_ref = refs[7 + len(_OUT_W)]
    out_ref[...] = _out_proj(x_ref[...], d_ref[...], o_ref_in[...], rs_ref[...], gc_ref[...], gg_ref[...],
                             gt_ref[...], w)


def _mix_s_out(x, d, o, rs, gc, gg, gt, w):
    return pl.pallas_call(
        _mix_s_out_kernel,
        out_shape=jax.ShapeDtypeStruct(x.shape, f32),
        compiler_params=pltpu.CompilerParams(vmem_limit_bytes=VMEM_LIMIT_BYTES),
        name="mix_sample_out",
    )(x, d, o, rs, gc, gg, gt, *[w[nm] for nm in _OUT_W])


def kernel(x_prompt, x_sample, c_prompt, c_sample, state_conv, state_gla, w_ada, b_ada, norm_ffn1, ffn1_gate, ffn1_up, ffn1_down, norm_mix, w_in, conv_w, conv_b, conv_ln_g, conv_ln_b, w_conv_out, w_alpha, b_alpha, gla_norm, w_gla_out, w_out, norm_ffn2, ffn2_gate, ffn2_up, ffn2_down, norm_final, w_ada_final, b_ada_final):
    nb, t, _ = x_prompt.shape
    ns = x_sample.shape[0]
    assert w_ada.shape[0] == 1 and x_sample.shape[1] == 1
    assert t % FFN_ROWS == 0 and t % MIX_ROWS == 0 and ns % CONV_S_ROWS == 0 and ns % GLA_S_ROWS == 0

    c_all = jnp.concatenate([c_prompt, c_sample], axis=0)
    mod = _ada(c_all, w_ada[0], b_ada[0])
    mod_f = _ada(c_all, w_ada_final, b_ada_final)
    mod_p = [m[:nb, None, :] for m in (*mod, *mod_f)]
    mod_s = [m[None, nb:, :] for m in (*mod, *mod_f)]
    sh1, sc1, gt1, sh2, sc2, gt2, sh3, sc3, gt3, shf, scf = range(11)

    row = lambda a: a.reshape(1, -1)
    cast = lambda a: a.astype(bf16)
    offs = [0]
    for width in IN_SPLITS:
        offs.append(offs[-1] + width)
    w_split = [w_in[0][:, offs[j]:offs[j + 1]] for j in range(len(IN_SPLITS))]
    pad_rank = LANES - GLA_RANK
    w = {
        "ca": cast(w_split[0]), "cb": cast(w_split[1]), "q": cast(w_split[2]), "k": cast(w_split[3]),
        "v": cast(w_split[4]), "r": cast(w_split[5]),
        "alr": cast(jnp.pad(w_split[6], ((0, 0), (0, pad_rank)))),
        "gc": cast(w_split[7]), "gg": cast(w_split[8]),
        "alpha": cast(jnp.pad(w_alpha[0], ((0, pad_rank), (0, 0)))), "b_alpha": row(b_alpha[0]),
        "ln_g": row(conv_ln_g[0]), "ln_b": row(conv_ln_b[0]), "conv_out": cast(w_conv_out[0]),
        "gla_norm": row(gla_norm[0]), "gla_out": cast(w_gla_out[0]), "out": cast(w_out[0]),
    }
    f1 = (cast(ffn1_gate[0]), cast(ffn1_up[0]), cast(ffn1_down[0]))
    f2 = (cast(ffn2_gate[0]), cast(ffn2_up[0]), cast(ffn2_down[0]))
    cw, cb = conv_w[0], row(conv_b[0])

    x1 = _ffn(x_prompt, [mod_p[sh1], mod_p[sc1], mod_p[gt1]], row(norm_ffn1[0]), *f1, rows=FFN_ROWS)
    x2, conv_p, gla_p = _mix_prompt(x1, [mod_p[sh2], mod_p[sc2], mod_p[gt2]], row(norm_mix[0]), cw, cb, w)
    y_prompt = _ffn(x2, [mod_p[sh3], mod_p[sc3], mod_p[gt3]], row(norm_ffn2[0]), *f2, rows=FFN_ROWS,
                    final_args=(row(norm_final), mod_p[shf], mod_p[scf]))

    xs = x_sample.reshape(1, ns, D_MODEL)
    xs1 = _ffn(xs, [mod_s[sh1], mod_s[sc1], mod_s[gt1]], row(norm_ffn1[0]), *f1, rows=ns)
    u, q, k, v, la, rs, gc, gg = _mix_s_in(xs1[0], mod_s[sh2][0], mod_s[sc2][0], row(norm_mix[0]), w)
    d, conv_s = _conv_s(state_conv[0].reshape(ns, (CONV_K - 1) * D_MODEL), u, cw, cb)
    o, gla_s = _gla_s(q, k, la, v, state_gla[0])
    xs2 = _mix_s_out(xs1[0], d, o, rs, gc, gg, mod_s[gt2][0], w)
    y_sample = _ffn(xs2[None], [mod_s[sh3], mod_s[sc3], mod_s[gt3]], row(norm_ffn2[0]), *f2, rows=ns,
                    final_args=(row(norm_final), mod_s[shf], mod_s[scf]))

    return (y_prompt, y_sample.reshape(ns, 1, D_MODEL), conv_p[None], gla_p[None],
            conv_s.reshape(1, ns, CONV_K - 1, D_MODEL), gla_s[None])
```
